```python
import jax, jax.numpy as jnp
from jax import lax
import numpy as np

D_MODEL = 1024
BATCH = 1
SEQ = 16384
DEPTH = 1
DEC_BATCH = 32
DEC_SEQ = 1
PAST_LEN = 16384
PAGE_SIZE = 128

SB_HEADS = 8
SB_HEAD_DIM = 64
SB_WIDTH = SB_HEADS * SB_HEAD_DIM
Q_BLOCK = 128
SB_BIAS_INIT = -8.0
RW_HEADS = 8
RW_HEAD_DIM = 64
RW_WIDTH = RW_HEADS * RW_HEAD_DIM
RANK_DECAY = 64
RANK_A = 64
RANK_GATE = 128
RW_FEATS = 3 * RW_WIDTH + RANK_DECAY + RANK_A + RANK_GATE
N_BRANCH = 2
P_IN = 3 * SB_WIDTH + RW_FEATS + N_BRANCH * D_MODEL
N_EXPERTS = 32
TOP_K = 4
D_FF = D_MODEL
SWIGLU_ALPHA = 1.702
SWIGLU_LIMIT = 7.0
ALPHA_RES = (2 * DEPTH) ** 0.25
BETA_INIT = (8 * DEPTH) ** -0.25
LN_EPS = 1e-5
GN_EPS = 64e-5

kernel_name = 'sb_rwkv7_gated_moe_decoder_step'


def layer_norm(x, g, b):
    xf = x.astype(jnp.float32)
    mu = jnp.mean(xf, axis=-1, keepdims=True)
    var = jnp.mean(jnp.square(xf - mu), axis=-1, keepdims=True)
    return ((xf - mu) * lax.rsqrt(var + LN_EPS) * g + b).astype(x.dtype)


def sb_block(q, k, v, q_pos, k_pos, bias):
    z = jnp.einsum('bqhd,bkhd->bhqk', q, k).astype(jnp.float32) * (SB_HEAD_DIM ** -0.5)
    z = z + bias.astype(jnp.float32)[None, :, None, None]
    valid = k_pos[None, :] < q_pos[:, None]
    u = jnp.where(valid, jax.nn.softplus(z), 0.0)
    tail = lax.cumsum(u, axis=3, reverse=True) - u
    a = jnp.where(valid, jnp.exp(jax.nn.log_sigmoid(z) - tail), 0.0)
    o = jnp.einsum('bhqk,bkhd->bqhd', a, v.astype(jnp.float32))
    return o.astype(q.dtype)


def sba_prompt(q, k, v, bias):
    B, S = q.shape[0], q.shape[1]
    k_pos = jnp.arange(S)

    def one_block(start):
        qb = lax.dynamic_slice_in_dim(q, start, Q_BLOCK, axis=1)
        return sb_block(qb, k, v, start + jnp.arange(Q_BLOCK), k_pos, bias)

    o = lax.map(one_block, jnp.arange(0, S, Q_BLOCK))
    return jnp.moveaxis(o, 0, 1).reshape(B, S, SB_WIDTH)


def make_sba_sample(k_past, v_past):
    def sba(q, k, v, bias):
        B, T = q.shape[0], q.shape[1]
        P = k_past.shape[1]
        kf = jnp.concatenate([k_past.astype(k.dtype), k], axis=1)
        vf = jnp.concatenate([v_past.astype(v.dtype), v], axis=1)
        o = sb_block(q, kf, vf, P + jnp.arange(T), jnp.arange(P + T), bias)
        return o.reshape(B, T, SB_WIDTH)
    return sba


def rwkv7_step(S, inp):
    r, w, k, v, kk, a = inp
    s_kk = jnp.einsum('bhvk,bhk->bhv', S, kk)
    S = (S * w[:, :, None, :]
         - s_kk[:, :, :, None] * (kk * a)[:, :, None, :]
         + v[:, :, :, None] * k[:, :, None, :])
    y = jnp.einsum('bhvk,bhk->bhv', S, r)
    return S, y


def rwkv7_branch(f_raw, shift0, wkv0, p):
    B, T, _ = f_raw.shape
    prev = jnp.concatenate([shift0[:, None, :].astype(f_raw.dtype), f_raw[:, :-1]], axis=1)
    f = f_raw + p['rw_mu'] * (prev - f_raw)
    c1, c2, c3 = RW_WIDTH, 2 * RW_WIDTH, 3 * RW_WIDTH
    fr, fk, fv, fw, fa, fg = jnp.split(f, [c1, c2, c3, c3 + RANK_DECAY, c3 + RANK_DECAY + RANK_A], axis=-1)

    def heads(t):
        return t.reshape(B, T, RW_HEADS, RW_HEAD_DIM).astype(jnp.float32)

    w_log = -jax.nn.softplus(-(p['rw_w0'] + jnp.tanh(fw) @ p['rw_w2'])) - 0.5
    decay = jnp.exp(-jnp.exp(w_log.astype(jnp.float32)))
    a = jax.nn.sigmoid(p['rw_a0'] + fa @ p['rw_a2'])
    g = jax.nn.sigmoid(fg) @ p['rw_g2']
    kk = heads(fk * p['rw_kk'])
    kk = kk * lax.rsqrt(jnp.maximum(jnp.sum(kk * kk, axis=-1, keepdims=True), 1e-24))
    k = fk * (1.0 + (a - 1.0) * p['rw_ka'])
    r_h, k_h, v_h, a_h, w_h = heads(fr), heads(k), heads(fv), heads(a), heads(decay)
    xs = tuple(jnp.moveaxis(t, 1, 0) for t in (r_h, w_h, k_h, v_h, kk, a_h))
    s_final, y = lax.scan(rwkv7_step, wkv0.astype(jnp.float32), xs)
    y = jnp.moveaxis(y, 0, 1)
    mu = jnp.mean(y, axis=-1, keepdims=True)
    var = jnp.mean(jnp.square(y - mu), axis=-1, keepdims=True)
    y = ((y - mu) * lax.rsqrt(var + GN_EPS)).reshape(B, T, RW_WIDTH) * p['rw_lnx_g'] + p['rw_lnx_b']
    bonus = jnp.sum(r_h * k_h * p['rw_rk'], axis=-1, keepdims=True) * v_h
    y = (y + bonus.reshape(B, T, RW_WIDTH)) * g
    return y.astype(f_raw.dtype), s_final.astype(wkv0.dtype), f_raw[:, -1]


def token_mixer(h, sba_fn, shift0, wkv0, p):
    B, T, _ = h.shape
    proj = h @ p['w_in']
    q, k, v, f_rw, gates = jnp.split(
        proj, [SB_WIDTH, 2 * SB_WIDTH, 3 * SB_WIDTH, 3 * SB_WIDTH + RW_FEATS], axis=-1)
    hd = (B, T, SB_HEADS, SB_HEAD_DIM)
    q, k, v = q.reshape(hd), k.reshape(hd), v.reshape(hd)
    o_a = sba_fn(q, k, v, p['sb_bias'])
    o_b, wkv_new, shift_new = rwkv7_branch(f_rw, shift0, wkv0, p)
    g_a, g_b = jnp.split(jax.nn.sigmoid(gates), 2, axis=-1)
    merged = g_a * (o_a @ p['w_branch_a']) + g_b * (o_b @ p['w_branch_b'])
    return merged @ p['w_out'], (k, v, wkv_new, shift_new)


def moe(h, p):
    B, T, D = h.shape
    xt = h.reshape(B * T, D)
    logits = (xt @ p['router_w'] + p['router_b']).astype(jnp.float32)
    top_val, top_idx = lax.top_k(logits, TOP_K)
    probs = jax.nn.softmax(top_val, axis=-1)
    combine = jnp.sum(jax.nn.one_hot(top_idx, N_EXPERTS, dtype=jnp.float32) * probs[..., None], axis=1)

    def expert(acc, ew):
        w1, b1, w2, b2, ce = ew
        hid = xt @ w1 + b1
        x_glu = jnp.minimum(hid[:, 0::2], SWIGLU_LIMIT)
        x_lin = jnp.clip(hid[:, 1::2], -SWIGLU_LIMIT, SWIGLU_LIMIT)
        act = x_glu * jax.nn.sigmoid(SWIGLU_ALPHA * x_glu) * (x_lin + 1.0)
        return acc + ce[:, None] * (act @ w2 + b2), None

    acc0 = jnp.zeros((B * T, D), jnp.float32)
    out, _ = lax.scan(expert, acc0, (p['exp_w1'], p['exp_b1'], p['exp_w2'], p['exp_b2'], combine.T))
    return out.reshape(B, T, D).astype(h.dtype)


def trunk_layer(x, c, sba_fn, shift0, wkv0, p):
    mod = jax.nn.silu(c) @ p['w_ada'] + p['b_ada']
    sh_m, sc_m, gt_m, sh_f, sc_f, gt_f = [m[:, None, :] for m in jnp.split(mod, 6, axis=-1)]
    h = x * (1.0 + sc_m) + sh_m
    mix, state = token_mixer(h, sba_fn, shift0, wkv0, p)
    x = layer_norm(ALPHA_RES * x + gt_m * mix, p['ln1_g'], p['ln1_b'])
    h = x * (1.0 + sc_f) + sh_f
    x = layer_norm(ALPHA_RES * x + gt_f * moe(h, p), p['ln2_g'], p['ln2_b'])
    return x, state


def setup_inputs(seed: int = 0) -> dict:
    key = jax.random.key(seed)
    ks = jax.random.split(key, 40)
    f32 = jnp.float32
    L = DEPTH
    n_pages = PAST_LEN // PAGE_SIZE
    n_used = DEC_BATCH * n_pages
    n_phys = n_used + n_used // 4

    def nrm(k, shape, scale):
        return jax.random.normal(k, shape, f32) * scale

    page_table = jax.random.permutation(ks[8], n_phys)[:n_used].reshape(DEC_BATCH, n_pages).astype(jnp.int32)
    return {
        'x_prompt': nrm(ks[0], (BATCH, SEQ, D_MODEL), 1.0),
        'x_sample': nrm(ks[1], (DEC_BATCH, DEC_SEQ, D_MODEL), 1.0),
        'c_prompt': nrm(ks[2], (BATCH, D_MODEL), 1.0),
        'c_sample': nrm(ks[3], (DEC_BATCH, D_MODEL), 1.0),
        'cache_k': nrm(ks[4], (L, n_phys, PAGE_SIZE, SB_HEADS, SB_HEAD_DIM), 1.0),
        'cache_v': nrm(ks[5], (L, n_phys, PAGE_SIZE, SB_HEADS, SB_HEAD_DIM), 1.0),
        'state_wkv': nrm(ks[6], (L, DEC_BATCH, RW_HEADS, RW_HEAD_DIM, RW_HEAD_DIM), 0.5),
        'state_shift': nrm(ks[7], (L, DEC_BATCH, RW_FEATS), 1.0),
        'page_table': page_table,
        'w_ada': nrm(ks[9], (L, D_MODEL, 6 * D_MODEL), D_MODEL ** -0.5),
        'b_ada': nrm(ks[10], (L, 6 * D_MODEL), 0.02),
        'w_in': nrm(ks[11], (L, D_MODEL, P_IN), D_MODEL ** -0.5),
        'sb_bias': SB_BIAS_INIT + nrm(ks[36], (L, SB_HEADS), 0.1),
        'rw_mu': jax.random.uniform(ks[12], (L, RW_FEATS), f32),
        'rw_w0': jax.random.uniform(ks[13], (L, RW_WIDTH), f32, -5.0, -1.0),
        'rw_w2': nrm(ks[14], (L, RANK_DECAY, RW_WIDTH), 0.1 * RANK_DECAY ** -0.5),
        'rw_a0': nrm(ks[15], (L, RW_WIDTH), 0.5),
        'rw_a2': nrm(ks[16], (L, RANK_A, RW_WIDTH), 0.1 * RANK_A ** -0.5),
        'rw_g2': nrm(ks[17], (L, RANK_GATE, RW_WIDTH), RANK_GATE ** -0.5),
        'rw_kk': 0.85 + nrm(ks[18], (L, RW_WIDTH), 0.05),
        'rw_ka': 1.0 + nrm(ks[19], (L, RW_WIDTH), 0.05),
        'rw_rk': nrm(ks[20], (L, RW_HEADS, RW_HEAD_DIM), 0.1),
        'rw_lnx_g': 1.0 + nrm(ks[21], (L, RW_WIDTH), 0.02),
        'rw_lnx_b': nrm(ks[22], (L, RW_WIDTH), 0.02),
        'w_branch_a': nrm(ks[23], (L, SB_WIDTH, D_MODEL), SB_WIDTH ** -0.5),
        'w_branch_b': nrm(ks[24], (L, RW_WIDTH, D_MODEL), RW_WIDTH ** -0.5),
        'w_out': nrm(ks[25], (L, D_MODEL, D_MODEL), BETA_INIT * D_MODEL ** -0.5),
        'ln1_g': 1.0 + nrm(ks[26], (L, D_MODEL), 0.02),
        'ln1_b': nrm(ks[27], (L, D_MODEL), 0.02),
        'ln2_g': 1.0 + nrm(ks[28], (L, D_MODEL), 0.02),
        'ln2_b': nrm(ks[29], (L, D_MODEL), 0.02),
        'router_w': nrm(ks[30], (L, D_MODEL, N_EXPERTS), D_MODEL ** -0.5),
        'router_b': nrm(ks[31], (L, N_EXPERTS), 0.01),
        'exp_w1': nrm(ks[32], (L, N_EXPERTS, D_MODEL, 2 * D_FF), D_MODEL ** -0.5),
        'exp_b1': nrm(ks[33], (L, N_EXPERTS, 2 * D_FF), 0.02),
        'exp_w2': nrm(ks[34], (L, N_EXPERTS, D_FF, D_MODEL), BETA_INIT * D_FF ** -0.5),
        'exp_b2': nrm(ks[35], (L, N_EXPERTS, D_MODEL), 0.02),
    }


def reference(x_prompt, x_sample, c_prompt, c_sample, cache_k, cache_v, state_wkv, state_shift, page_table,
              w_ada, b_ada, w_in, sb_bias, rw_mu, rw_w0, rw_w2, rw_a0, rw_a2, rw_g2, rw_kk, rw_ka, rw_rk,
              rw_lnx_g, rw_lnx_b, w_branch_a, w_branch_b, w_out, ln1_g, ln1_b, ln2_g, ln2_b,
              router_w, router_b, exp_w1, exp_b1, exp_w2, exp_b2):
    B = x_prompt.shape[0]
    DB = x_sample.shape[0]
    n_pages = page_table.shape[1]
    xp, xs = x_prompt, x_sample
    kp_l, vp_l, wp_l, sp_l = [], [], [], []
    ks_l, vs_l, ws_l, ss_l = [], [], [], []
    for l in range(DEPTH):
        p = {
            'w_ada': w_ada[l], 'b_ada': b_ada[l], 'w_in': w_in[l], 'sb_bias': sb_bias[l],
            'rw_mu': rw_mu[l], 'rw_w0': rw_w0[l], 'rw_w2': rw_w2[l], 'rw_a0': rw_a0[l], 'rw_a2': rw_a2[l],
            'rw_g2': rw_g2[l], 'rw_kk': rw_kk[l], 'rw_ka': rw_ka[l], 'rw_rk': rw_rk[l],
            'rw_lnx_g': rw_lnx_g[l], 'rw_lnx_b': rw_lnx_b[l],
            'w_branch_a': w_branch_a[l], 'w_branch_b': w_branch_b[l], 'w_out': w_out[l],
            'ln1_g': ln1_g[l], 'ln1_b': ln1_b[l], 'ln2_g': ln2_g[l], 'ln2_b': ln2_b[l],
            'router_w': router_w[l], 'router_b': router_b[l],
            'exp_w1': exp_w1[l], 'exp_b1': exp_b1[l], 'exp_w2': exp_w2[l], 'exp_b2': exp_b2[l],
        }
        shift0 = jnp.zeros((B, RW_FEATS), xp.dtype)
        wkv0 = jnp.zeros((B, RW_HEADS, RW_HEAD_DIM, RW_HEAD_DIM), xp.dtype)
        xp, (k_new, v_new, wkv_new, sh_new) = trunk_layer(xp, c_prompt, sba_prompt, shift0, wkv0, p)
        kp_l.append(k_new); vp_l.append(v_new); wp_l.append(wkv_new); sp_l.append(sh_new)
        k_past = cache_k[l][page_table].reshape(DB, n_pages * PAGE_SIZE, SB_HEADS, SB_HEAD_DIM)
        v_past = cache_v[l][page_table].reshape(DB, n_pages * PAGE_SIZE, SB_HEADS, SB_HEAD_DIM)
        xs, (k_new, v_new, wkv_new, sh_new) = trunk_layer(
            xs, c_sample, make_sba_sample(k_past, v_past), state_shift[l], state_wkv[l], p)
        ks_l.append(k_new); vs_l.append(v_new); ws_l.append(wkv_new); ss_l.append(sh_new)
    new_k_prompt, new_v_prompt = jnp.stack(kp_l), jnp.stack(vp_l)
    new_wkv_prompt, new_shift_prompt = jnp.stack(wp_l), jnp.stack(sp_l)
    new_k_sample, new_v_sample = jnp.stack(ks_l), jnp.stack(vs_l)
    new_wkv_sample, new_shift_sample = jnp.stack(ws_l), jnp.stack(ss_l)
    return (xp, xs, new_k_prompt, new_v_prompt, new_wkv_prompt, new_shift_prompt,
            new_k_sample, new_v_sample, new_wkv_sample, new_shift_sample)
```

```python
import functools

import jax
import jax.numpy as jnp
from jax import lax
from jax.experimental import pallas as pl
from jax.experimental.pallas import tpu as pltpu

F32 = jnp.float32
BF16 = jnp.bfloat16

TOP_K = 4
SWIGLU_ALPHA = 1.702
SWIGLU_LIMIT = 7.0
LN_EPS = 1e-5
GN_EPS = 64e-5
KK_NORM_FLOOR = 1e-24
LOG_DECAY_OFFSET = 0.5

V7X_LANES = 128
V7X_SUBLANES = 8
V7X_VMEM_LIMIT_BYTES = 56 * 1024 * 1024

RW_CHUNK = 64
SBA_TILE = 256


def _cparams(*sem):
    return pltpu.CompilerParams(dimension_semantics=sem, vmem_limit_bytes=V7X_VMEM_LIMIT_BYTES)


def _dot(a, b):
    return jnp.dot(a.astype(BF16), b.astype(BF16), preferred_element_type=F32)


def _dot_nt(a, b):
    return lax.dot_general(a.astype(BF16), b.astype(BF16), (((1,), (1,)), ((), ())),
                           preferred_element_type=F32)


def _dot_tn(a, b):
    return lax.dot_general(a.astype(BF16), b.astype(BF16), (((0,), (0,)), ((), ())),
                           preferred_element_type=F32)


def _dot_f32(a, b):
    return jnp.dot(a, b, preferred_element_type=F32, precision=lax.Precision.HIGHEST)


def _split3(x):
    hi = x.astype(BF16)
    r1 = x - hi.astype(F32)
    mid = r1.astype(BF16)
    lo = (r1 - mid.astype(F32)).astype(BF16)
    return hi, mid, lo


def _dot_exact_lhs(m_bf16, x):
    hi, mid, lo = _split3(x)
    d = functools.partial(jnp.dot, preferred_element_type=F32)
    return d(m_bf16, hi) + d(m_bf16, mid) + d(m_bf16, lo)


def _dot_exact_rhs(x, m_bf16):
    hi, mid, lo = _split3(x)
    d = functools.partial(jnp.dot, preferred_element_type=F32)
    return d(hi, m_bf16) + d(mid, m_bf16) + d(lo, m_bf16)


def _softplus(z):
    return jnp.maximum(z, 0.0) + jnp.log(1.0 + jnp.exp(-jnp.abs(z)))


def _sigmoid(z):
    return 1.0 / (1.0 + jnp.exp(-z))


def _layer_norm(y, g, b):
    mu = jnp.mean(y, axis=-1, keepdims=True)
    yc = y - mu
    var = jnp.mean(yc * yc, axis=-1, keepdims=True)
    return yc * lax.rsqrt(var + LN_EPS) * g + b


def _iota2(shape, dim):
    return lax.broadcasted_iota(jnp.int32, shape, dim)


def _adaln_kernel(c_ref, w_ref, b_ref, o_ref):
    c = c_ref[...]
    s = c * _sigmoid(c)
    o_ref[...] = _dot(s, w_ref[...]) + b_ref[...]


def _adaln(c, w_ada, b_ada):
    rows, d = c.shape
    n = w_ada.shape[1]
    bn = d
    return pl.pallas_call(
        _adaln_kernel,
        grid=(n // bn,),
        in_specs=[pl.BlockSpec((rows, d), lambda j: (0, 0)),
                  pl.BlockSpec((d, bn), lambda j: (0, j)),
                  pl.BlockSpec((1, bn), lambda j: (0, j))],
        out_specs=pl.BlockSpec((rows, bn), lambda j: (0, j)),
        out_shape=jax.ShapeDtypeStruct((rows, n), F32),
        compiler_params=_cparams("parallel"),
        name="adaln",
    )(c, w_ada, b_ada.reshape(1, n))


def _mod_spec(per_row, tm, d, toks_per_batch):
    if per_row:
        return pl.BlockSpec((tm, d), lambda i: (i, 0))
    return pl.BlockSpec((None, 1, d), lambda i: ((i * tm) // toks_per_batch, 0, 0))


def _inproj_kernel(x_ref, sc_ref, sh_ref, w_ref, q_ref, k_ref, v_ref, f_ref, *, sbw, q_scale):
    h = (x_ref[...] * (1.0 + sc_ref[...]) + sh_ref[...]).astype(BF16)
    d = functools.partial(jnp.dot, preferred_element_type=F32)
    q_ref[...] = (d(h, w_ref[:, 0:sbw]) * q_scale).astype(BF16)
    k_ref[...] = d(h, w_ref[:, sbw:2 * sbw])
    v_ref[...] = d(h, w_ref[:, 2 * sbw:3 * sbw])
    f_ref[...] = d(h, w_ref[:, 3 * sbw:])


def _inproj(x, sc, sh, w_qkvf, *, sbw, q_scale, per_row, toks_per_batch, tm):
    n, d = x.shape
    p = w_qkvf.shape[1]
    nf = p - 3 * sbw
    mspec = _mod_spec(per_row, tm, d, toks_per_batch)
    return pl.pallas_call(
        functools.partial(_inproj_kernel, sbw=sbw, q_scale=q_scale),
        grid=(n // tm,),
        in_specs=[pl.BlockSpec((tm, d), lambda i: (i, 0)), mspec, mspec,
                  pl.BlockSpec((d, p), lambda i: (0, 0))],
        out_specs=[pl.BlockSpec((tm, sbw), lambda i: (i, 0)),
                   pl.BlockSpec((tm, sbw), lambda i: (i, 0)),
                   pl.BlockSpec((tm, sbw), lambda i: (i, 0)),
                   pl.BlockSpec((tm, nf), lambda i: (i, 0))],
        out_shape=[jax.ShapeDtypeStruct((n, sbw), BF16),
                   jax.ShapeDtypeStruct((n, sbw), F32),
                   jax.ShapeDtypeStruct((n, sbw), F32),
                   jax.ShapeDtypeStruct((n, nf), F32)],
        compiler_params=_cparams("parallel"),
        name="inproj",
    )(x, sc, sh, w_qkvf)


def _sba_prompt_kernel(bias_ref, qT_ref, k_ref, vT_ref, oT_ref, *, tile):
    h = pl.program_id(1)
    qi = pl.program_id(2)
    bias = bias_ref[h]
    qT = qT_ref[...]
    hd = qT.shape[0]
    row = _iota2((tile, tile), 0)
    col = _iota2((tile, tile), 1)
    tri = (col >= row).astype(BF16)
    causal = row < col

    def one_tile(kb, carry, acc, masked):
        z = jnp.dot(k_ref[kb], qT, preferred_element_type=F32) + bias
        u = _softplus(z)
        if masked:
            u = jnp.where(causal, u, 0.0)
        c = jnp.dot(tri, u.astype(BF16), preferred_element_type=F32)
        a = jnp.exp(z - c)
        if masked:
            a = jnp.where(causal, a, 0.0)
        p = jnp.dot(vT_ref[kb], a.astype(BF16), preferred_element_type=F32)
        acc = acc + p * jnp.exp(-carry)
        return carry + c[0:1, :], acc

    carry0 = jnp.zeros((1, tile), F32)
    acc0 = jnp.zeros((hd, tile), F32)
    carry, acc = one_tile(qi, carry0, acc0, True)

    def body(j, st):
        return one_tile(qi - 1 - j, st[0], st[1], False)

    carry, acc = lax.fori_loop(0, qi, body, (carry, acc))
    oT_ref[...] = acc


def _sba_prompt(q, k, v, bias, *, heads, tile):
    b, s, w = q.shape
    hd = w // heads
    nt = s // tile
    k5 = k.astype(BF16).reshape(b, nt, tile, heads, hd).transpose(0, 3, 1, 2, 4)
    q5 = q.reshape(b, nt, tile, heads, hd).transpose(0, 3, 1, 4, 2)
    v5 = v.astype(BF16).reshape(b, nt, tile, heads, hd).transpose(0, 3, 1, 4, 2)
    oT = pl.pallas_call(
        functools.partial(_sba_prompt_kernel, tile=tile),
        grid_spec=pltpu.PrefetchScalarGridSpec(
            num_scalar_prefetch=0,
            grid=(b, heads, nt),
            in_specs=[pl.BlockSpec(memory_space=pltpu.SMEM),
                      pl.BlockSpec((None, None, None, hd, tile), lambda bi, h, i: (bi, h, i, 0, 0)),
                      pl.BlockSpec((None, None, nt, tile, hd), lambda bi, h, i: (bi, h, 0, 0, 0)),
                      pl.BlockSpec((None, None, nt, hd, tile), lambda bi, h, i: (bi, h, 0, 0, 0))],
            out_specs=pl.BlockSpec((None, None, None, hd, tile), lambda bi, h, i: (bi, h, i, 0, 0)),
        ),
        out_shape=jax.ShapeDtypeStruct((b, heads, nt, hd, tile), F32),
        compiler_params=_cparams("parallel", "parallel", "arbitrary"),
        name="sba_prompt",
    )(bias, q5, k5, v5)
    return oT.transpose(0, 2, 4, 1, 3).reshape(b, s, w)


def _sba_sample_kernel(pt_ref, q_ref, bias_ref, k_ref, v_ref, o_ref, carry_ref, acc_ref, *, heads):
    j = pl.program_id(1)
    page, w = k_ref.shape
    hd = w // heads

    @pl.when(j == 0)
    def _():
        carry_ref[...] = jnp.zeros_like(carry_ref)
        acc_ref[...] = jnp.zeros_like(acc_ref)

    head_of_lane = _iota2((heads, w), 1) // hd
    own = head_of_lane == _iota2((heads, w), 0)
    qe = jnp.where(own, jnp.broadcast_to(q_ref[...], (heads, w)), 0.0)
    z = _dot_nt(qe, k_ref[...]) + bias_ref[...]
    u = _softplus(z)
    tri = (_iota2((page, page), 0) >= _iota2((page, page), 1)).astype(BF16)
    c = jnp.dot(u.astype(BF16), tri, preferred_element_type=F32)
    a = jnp.exp(z - c)
    p = _dot(a, v_ref[...])
    carry = carry_ref[...]
    acc_ref[...] += p * jnp.exp(-carry)
    carry_ref[...] = carry + c[:, 0:1]

    @pl.when(j == pl.num_programs(1) - 1)
    def _():
        o_ref[...] = jnp.sum(jnp.where(own, acc_ref[...], 0.0), axis=0, keepdims=True)


def _sba_sample(q, cache_k, cache_v, page_table, bias, *, heads):
    db, w = q.shape
    n_pages = page_table.shape[1]
    page = cache_k.shape[1]

    def page_map(b, j, pt):
        return (pt[b, n_pages - 1 - j], 0, 0)

    o = pl.pallas_call(
        functools.partial(_sba_sample_kernel, heads=heads),
        grid_spec=pltpu.PrefetchScalarGridSpec(
            num_scalar_prefetch=1,
            grid=(db, n_pages),
            in_specs=[pl.BlockSpec((None, 1, w), lambda b, j, pt: (b, 0, 0)),
                      pl.BlockSpec((heads, 1), lambda b, j, pt: (0, 0)),
                      pl.BlockSpec((None, page, w), page_map),
                      pl.BlockSpec((None, page, w), page_map)],
            out_specs=pl.BlockSpec((None, 1, w), lambda b, j, pt: (b, 0, 0)),
            scratch_shapes=[pltpu.VMEM((heads, 1), F32), pltpu.VMEM((heads, w), F32)],
        ),
        out_shape=jax.ShapeDtypeStruct((db, 1, w), F32),
        compiler_params=_cparams("parallel", "arbitrary"),
        name="sba_sample",
    )(page_table, q.reshape(db, 1, w), bias.reshape(heads, 1), cache_k, cache_v)
    return o.reshape(db, w)


def _rw_features(f_ref, prev_ref, mu_ref, w0_ref, w2_ref, a0_ref, a2_ref, g2_ref, kkp_ref, ka_ref, rk_ref,
                 *, rww, rank_w, rank_a, hd):
    f = f_ref[...]
    fm = f + mu_ref[...] * (prev_ref[...] - f)
    fr = fm[:, 0:rww]
    fk = fm[:, rww:2 * rww]
    fv = fm[:, 2 * rww:3 * rww]
    o = 3 * rww
    fw = fm[:, o:o + rank_w]
    fa = fm[:, o + rank_w:o + rank_w + rank_a]
    fg = fm[:, o + rank_w + rank_a:]
    w_log = -_softplus(-(w0_ref[...] + _dot(jnp.tanh(fw), w2_ref[...]))) - LOG_DECAY_OFFSET
    lw = -jnp.exp(w_log)
    a = _sigmoid(a0_ref[...] + _dot(fa, a2_ref[...]))
    g = _dot(_sigmoid(fg), g2_ref[...])
    ones_blk = (_iota2((rww, rww), 0) // hd == _iota2((rww, rww), 1) // hd).astype(BF16)
    kk = fk * kkp_ref[...]
    kk = kk * lax.rsqrt(jnp.maximum(_dot_exact_rhs(kk * kk, ones_blk), KK_NORM_FLOOR))
    k = fk * (1.0 + (a - 1.0) * ka_ref[...])
    bonus = _dot_exact_rhs(fr * k * rk_ref[...], ones_blk) * fv
    return fr, lw, k, fv, kk, kk * a, g, bonus


def _rw_prep_step_kernel(f_ref, prev_ref, mu_ref, w0_ref, w2_ref, a0_ref, a2_ref, g2_ref, kkp_ref, ka_ref, rk_ref,
                         r_ref, w_ref, k_ref, v_ref, kk_ref, b_ref, g_ref, bonus_ref, **dims):
    r, lw, k, v, kk, beta, g, bonus = _rw_features(
        f_ref, prev_ref, mu_ref, w0_ref, w2_ref, a0_ref, a2_ref, g2_ref, kkp_ref, ka_ref, rk_ref, **dims)
    r_ref[...] = r
    w_ref[...] = jnp.exp(lw)
    k_ref[...] = k
    v_ref[...] = v
    kk_ref[...] = kk
    b_ref[...] = beta
    g_ref[...] = g
    bonus_ref[...] = bonus


def _rw_prep_chunk_kernel(f_ref, prev_ref, mu_ref, w0_ref, w2_ref, a0_ref, a2_ref, g2_ref, kkp_ref, ka_ref, rk_ref,
                          rt_ref, at_ref, kh_ref, bh_ref, kb_ref, bb_ref, v_ref, pc_ref, g_ref, bonus_ref,
                          *, chunk, **dims):
    r, lw, k, v, kk, beta, g, bonus = _rw_features(
        f_ref, prev_ref, mu_ref, w0_ref, w2_ref, a0_ref, a2_ref, g2_ref, kkp_ref, ka_ref, rk_ref, **dims)
    tm = r.shape[0]
    hd = dims["hd"]
    heads = r.shape[1] // hd
    row = _iota2((tm, tm), 0)
    col = _iota2((tm, tm), 1)
    same = (row // chunk) == (col // chunk)
    lp = _dot_exact_lhs((same & (col <= row)).astype(BF16), lw)
    lpc = _dot_exact_lhs(same.astype(BF16), lw)
    e_inv = jnp.exp(-lp)
    e_end = jnp.exp(lpc - lp)
    outs = ((rt_ref, r * jnp.exp(lp)), (at_ref, kk * jnp.exp(lp - lw)),
            (kh_ref, k * e_inv), (bh_ref, beta * e_inv),
            (kb_ref, k * e_end), (bb_ref, beta * e_end),
            (v_ref, v), (pc_ref, jnp.exp(lpc)), (g_ref, g), (bonus_ref, bonus))
    for ref, val in outs:
        for h in range(heads):
            ref[h] = val[:, h * hd:(h + 1) * hd].astype(ref.dtype)


def _rw_param_specs(nfeat, rww, rank_w, rank_a, rank_g):
    c = lambda i: (0, 0)
    return [pl.BlockSpec((1, nfeat), c), pl.BlockSpec((1, rww), c), pl.BlockSpec((rank_w, rww), c),
            pl.BlockSpec((1, rww), c), pl.BlockSpec((rank_a, rww), c), pl.BlockSpec((rank_g, rww), c),
            pl.BlockSpec((1, rww), c), pl.BlockSpec((1, rww), c), pl.BlockSpec((1, rww), c)]


def _rw_param_args(p):
    rww = p["rw_w0"].shape[0]
    return (p["rw_mu"].reshape(1, -1), p["rw_w0"].reshape(1, rww), p["rw_w2"].astype(BF16),
            p["rw_a0"].reshape(1, rww), p["rw_a2"].astype(BF16), p["rw_g2"].astype(BF16),
            p["rw_kk"].reshape(1, rww), p["rw_ka"].reshape(1, rww), p["rw_rk"].reshape(1, rww))


def _rw_dims(p):
    rww = p["rw_w0"].shape[0]
    return dict(rww=rww, rank_w=p["rw_w2"].shape[0], rank_a=p["rw_a2"].shape[0], hd=p["rw_rk"].shape[1])


def _rw_prep_step(f, prev, p):
    n, nfeat = f.shape
    dims = _rw_dims(p)
    rww = dims["rww"]
    row = pl.BlockSpec((n, rww), lambda i: (0, 0))
    return pl.pallas_call(
        functools.partial(_rw_prep_step_kernel, **dims),
        grid=(1,),
        in_specs=[pl.BlockSpec((n, nfeat), lambda i: (0, 0))] * 2
        + _rw_param_specs(nfeat, rww, dims["rank_w"], dims["rank_a"], p["rw_g2"].shape[0]),
        out_specs=[row] * 8,
        out_shape=[jax.ShapeDtypeStruct((n, rww), F32)] * 8,
        compiler_params=_cparams("arbitrary"),
        name="rw_prep_step",
    )(f, prev, *_rw_param_args(p))


def _rw_prep_chunk(f, prev, p, *, tm):
    n, nfeat = f.shape
    dims = _rw_dims(p)
    rww, hd = dims["rww"], dims["hd"]
    heads = rww // hd
    hm = pl.BlockSpec((heads, tm, hd), lambda i: (0, i, 0))
    dts = [BF16] * 7 + [F32] * 3
    return pl.pallas_call(
        functools.partial(_rw_prep_chunk_kernel, chunk=RW_CHUNK, **dims),
        grid=(n // tm,),
        in_specs=[pl.BlockSpec((tm, nfeat), lambda i: (i, 0))] * 2
        + _rw_param_specs(nfeat, rww, dims["rank_w"], dims["rank_a"], p["rw_g2"].shape[0]),
        out_specs=[hm] * 10,
        out_shape=[jax.ShapeDtypeStruct((heads, n, hd), dt) for dt in dts],
        compiler_params=_cparams("parallel"),
        name="rw_prep_chunk",
    )(f, prev, *_rw_param_args(p))


def _unit_lower_inverse(l_strict, n):
    eye = (_iota2((n, n), 0) == _iota2((n, n), 1)).astype(F32)
    t = eye - l_strict
    pw = _dot_f32(l_strict, l_strict)
    span = 2
    while span < n:
        t = t + _dot_f32(t, pw)
        span *= 2
        if span < n:
            pw = _dot_f32(pw, pw)
    return t


def _rw_chunk_kernel(st0_ref, rt_ref, at_ref, kh_ref, bh_ref, kb_ref, bb_ref, v_ref, pc_ref, g_ref, bonus_ref,
                     lng_ref, lnb_ref, o_ref, st_ref, st_scr, *, heads):
    c = pl.program_id(1)
    n = rt_ref.shape[1]

    @pl.when(c == 0)
    def _():
        st_scr[...] = st0_ref[...]

    row = _iota2((n, n), 0)
    col = _iota2((n, n), 1)
    strict = col < row
    incl = col <= row
    for h in range(heads):
        rt, at, kh, bh, kb, bb, v = (r[h] for r in (rt_ref, at_ref, kh_ref, bh_ref, kb_ref, bb_ref, v_ref))
        a_ab = jnp.where(strict, _dot_nt(at, bh), 0.0)
        a_ak = jnp.where(strict, _dot_nt(at, kh), 0.0)
        a_rk = jnp.where(incl, _dot_nt(rt, kh), 0.0)
        a_rb = jnp.where(incl, _dot_nt(rt, bh), 0.0)
        t = _unit_lower_inverse(a_ab, n)
        u = _dot(t, _dot(a_ak, v))
        w = _dot(t, at)
        s = st_scr[h]
        x = -(u + _dot_nt(w, s))
        y = _dot_nt(rt, s) + _dot(a_rk, v) + _dot(a_rb, x)
        st_scr[h] = s * pc_ref[h][0:1, :] + _dot_tn(v, kb) + _dot_tn(x, bb)
        mu = jnp.mean(y, axis=-1, keepdims=True)
        yc = y - mu
        var = jnp.mean(yc * yc, axis=-1, keepdims=True)
        yn = yc * lax.rsqrt(var + GN_EPS) * lng_ref[h] + lnb_ref[h]
        o_ref[h] = ((yn + bonus_ref[h]) * g_ref[h]).astype(o_ref.dtype)

    @pl.when(c == pl.num_programs(1) - 1)
    def _():
        st_ref[...] = st_scr[...]


def _rw_chunk(st0, ops, lng, lnb, *, batch):
    heads, n, hd = ops[0].shape
    t = n // batch
    nc = t // RW_CHUNK
    blk = pl.BlockSpec((heads, RW_CHUNK, hd), lambda b, c: (0, b * nc + c, 0))
    stspec = pl.BlockSpec((None, heads, hd, hd), lambda b, c: (b, 0, 0, 0))
    pspec = pl.BlockSpec((heads, 1, hd), lambda b, c: (0, 0, 0))
    return pl.pallas_call(
        functools.partial(_rw_chunk_kernel, heads=heads),
        grid=(batch, nc),
        in_specs=[stspec] + [blk] * 10 + [pspec, pspec],
        out_specs=[blk, stspec],
        out_shape=[jax.ShapeDtypeStruct((heads, n, hd), BF16),
                   jax.ShapeDtypeStruct((batch, heads, hd, hd), F32)],
        scratch_shapes=[pltpu.VMEM((heads, hd, hd), F32)],
        compiler_params=_cparams("parallel", "arbitrary"),
        name="rw_chunk",
    )(st0, *ops, lng.reshape(heads, 1, hd), lnb.reshape(heads, 1, hd))


def _rw_step_kernel(s_ref, r_ref, w_ref, k_ref, v_ref, kk_ref, b_ref, g_ref, bonus_ref, lng_ref, lnb_ref,
                    o_ref, so_ref, *, heads):
    hd = s_ref.shape[-1]
    eye = _iota2((hd, hd), 0) == _iota2((hd, hd), 1)

    def col_of(rowvec):
        return jnp.sum(jnp.where(eye, jnp.broadcast_to(rowvec, (hd, hd)), 0.0), axis=1, keepdims=True)

    def row_of(colvec):
        return jnp.sum(jnp.where(eye, jnp.broadcast_to(colvec, (hd, hd)), 0.0), axis=0, keepdims=True)

    for h in range(heads):
        sl = slice(h * hd, (h + 1) * hd)
        r, w, k, v, kk, beta = (x[:, sl] for x in (r_ref[...], w_ref[...], k_ref[...], v_ref[...],
                                                     kk_ref[...], b_ref[...]))
        s = s_ref[h]
        s_kk = jnp.sum(s * kk, axis=1, keepdims=True)
        s_new = s * w - s_kk * beta + col_of(v) * k
        so_ref[h] = s_new
        y = row_of(jnp.sum(s_new * r, axis=1, keepdims=True))
        mu = jnp.mean(y, axis=-1, keepdims=True)
        yc = y - mu
        var = jnp.mean(yc * yc, axis=-1, keepdims=True)
        yn = yc * lax.rsqrt(var + GN_EPS) * lng_ref[:, sl] + lnb_ref[:, sl]
        o_ref[:, sl] = (yn + bonus_ref[...][:, sl]) * g_ref[...][:, sl]


def _rw_step(state, feats, lng, lnb):
    db, heads, hd, _ = state.shape
    w = heads * hd
    rowspec = pl.BlockSpec((None, 1, w), lambda b: (b, 0, 0))
    sspec = pl.BlockSpec((None, heads, hd, hd), lambda b: (b, 0, 0, 0))
    pspec = pl.BlockSpec((1, w), lambda b: (0, 0))
    o, s_new = pl.pallas_call(
        functools.partial(_rw_step_kernel, heads=heads),
        grid=(db,),
        in_specs=[sspec] + [rowspec] * 8 + [pspec, pspec],
        out_specs=[rowspec, sspec],
        out_shape=[jax.ShapeDtypeStruct((db, 1, w), F32), jax.ShapeDtypeStruct(state.shape, F32)],
        compiler_params=_cparams("parallel"),
        name="rw_step",
    )(state, *[x.reshape(db, 1, w) for x in feats], lng.reshape(1, w), lnb.reshape(1, w))
    return o.reshape(db, w), s_new


def _merge_kernel(x_ref, oa_ref, ob_ref, scm_ref, shm_ref, gtm_ref, scf_ref, shf_ref,
                  wg_ref, wa_ref, wb_ref, wo_ref, l1g_ref, l1b_ref, rw_ref, rb_ref,
                  x1_ref, h2_ref, comb_ref, *, alpha_res):
    x = x_ref[...]
    d = x.shape[1]
    h = (x * (1.0 + scm_ref[...]) + shm_ref[...]).astype(BF16)
    dd = functools.partial(jnp.dot, preferred_element_type=F32)
    g_a = _sigmoid(dd(h, wg_ref[:, 0:d]))
    g_b = _sigmoid(dd(h, wg_ref[:, d:]))
    merged = g_a * _dot(oa_ref[...], wa_ref[...]) + g_b * _dot(ob_ref[...], wb_ref[...])
    mix = _dot(merged, wo_ref[...])
    x1 = _layer_norm(alpha_res * x + gtm_ref[...] * mix, l1g_ref[...], l1b_ref[...])
    x1_ref[...] = x1
    h2 = x1 * (1.0 + scf_ref[...]) + shf_ref[...]
    h2_ref[...] = h2.astype(BF16)
    logits = _dot_f32(h2, rw_ref[...]) + rb_ref[...]
    ne = logits.shape[1]
    lane = _iota2(logits.shape, 1)
    rest = logits
    picks = []
    for _ in range(TOP_K):
        m = jnp.max(rest, axis=1, keepdims=True)
        first = jnp.min(jnp.where(rest == m, lane, ne), axis=1, keepdims=True)
        sel = lane == first
        picks.append((m, sel))
        rest = jnp.where(sel, -jnp.inf, rest)
    top = picks[0][0]
    es = [jnp.exp(m - top) for m, _ in picks]
    denom = es[0]
    for e in es[1:]:
        denom = denom + e
    comb = jnp.zeros_like(logits)
    for e, (_, sel) in zip(es, picks):
        comb = comb + jnp.where(sel, e / denom, 0.0)
    comb_ref[...] = comb


def _merge(x, o_a, o_b, mods, p, *, alpha_res, per_row, toks_per_batch, tm):
    n, d = x.shape
    sbw = o_a.shape[1]
    rww = o_b.shape[1]
    ne = p["router_w"].shape[1]
    mspec = _mod_spec(per_row, tm, d, toks_per_batch)
    c = lambda i: (0, 0)
    tok = lambda width: pl.BlockSpec((tm, width), lambda i: (i, 0))
    return pl.pallas_call(
        functools.partial(_merge_kernel, alpha_res=alpha_res),
        grid=(n // tm,),
        in_specs=[tok(d), tok(sbw), tok(rww)] + [mspec] * 5
        + [pl.BlockSpec((d, 2 * d), c), pl.BlockSpec((sbw, d), c), pl.BlockSpec((rww, d), c),
           pl.BlockSpec((d, d), c), pl.BlockSpec((1, d), c), pl.BlockSpec((1, d), c),
           pl.BlockSpec((d, ne), c), pl.BlockSpec((1, ne), c)],
        out_specs=[tok(d), tok(d), tok(ne)],
        out_shape=[jax.ShapeDtypeStruct((n, d), F32), jax.ShapeDtypeStruct((n, d), BF16),
                   jax.ShapeDtypeStruct((n, ne), F32)],
        compiler_params=_cparams("parallel"),
        name="merge",
    )(x, o_a, o_b, *mods, p["w_gates"], p["w_branch_a"], p["w_branch_b"], p["w_out"],
      p["ln1_g"].reshape(1, d), p["ln1_b"].reshape(1, d), p["router_w"], p["router_b"].reshape(1, ne))


def _moe_kernel(h_ref, comb_ref, x1_ref, gtf_ref, w1g_ref, w1l_ref, b1g_ref, b1l_ref, w2_ref, b2_ref,
                l2g_ref, l2b_ref, o_ref, acc_ref, *, alpha_res):
    e = pl.program_id(1)

    @pl.when(e == 0)
    def _():
        acc_ref[...] = jnp.zeros_like(acc_ref)

    h = h_ref[...]
    dd = functools.partial(jnp.dot, preferred_element_type=F32)
    x_glu = jnp.minimum(dd(h, w1g_ref[...]) + b1g_ref[...], SWIGLU_LIMIT)
    x_lin = jnp.clip(dd(h, w1l_ref[...]) + b1l_ref[...], -SWIGLU_LIMIT, SWIGLU_LIMIT)
    act = x_glu * _sigmoid(SWIGLU_ALPHA * x_glu) * (x_lin + 1.0)
    y = _dot(act, w2_ref[...]) + b2_ref[...]
    comb = comb_ref[...]
    ce = jnp.sum(jnp.where(_iota2(comb.shape, 1) == e, comb, 0.0), axis=1, keepdims=True)
    acc_ref[...] += ce * y

    @pl.when(e == pl.num_programs(1) - 1)
    def _():
        o_ref[...] = _layer_norm(alpha_res * x1_ref[...] + gtf_ref[...] * acc_ref[...],
                                 l2g_ref[...], l2b_ref[...])


def _moe(h2, comb, x1, gtf, p, *, alpha_res, per_row, toks_per_batch, tm):
    n, d = x1.shape
    ne, _, ff = p["w1g"].shape
    if per_row:
        mspec = pl.BlockSpec((tm, d), lambda i, e: (i, 0))
    else:
        mspec = pl.BlockSpec((None, 1, d), lambda i, e: ((i * tm) // toks_per_batch, 0, 0))
    tok = lambda width: pl.BlockSpec((tm, width), lambda i, e: (i, 0))
    ex = lambda a, b: pl.BlockSpec((None, a, b), lambda i, e: (e, 0, 0))
    c = lambda i, e: (0, 0)
    return pl.pallas_call(
        functools.partial(_moe_kernel, alpha_res=alpha_res),
        grid=(n // tm, ne),
        in_specs=[tok(d), tok(ne), tok(d), mspec, ex(d, ff), ex(d, ff), ex(1, ff), ex(1, ff),
                  ex(ff, d), ex(1, d), pl.BlockSpec((1, d), c), pl.BlockSpec((1, d), c)],
        out_specs=tok(d),
        out_shape=jax.ShapeDtypeStruct((n, d), F32),
        scratch_shapes=[pltpu.VMEM((tm, d), F32)],
        compiler_params=_cparams("parallel", "arbitrary"),
        name="moe",
    )(h2, comb, x1, gtf, p["w1g"], p["w1l"], p["b1g"], p["b1l"], p["w2"], p["b2"],
      p["ln2_g"].reshape(1, d), p["ln2_b"].reshape(1, d))


def _layer_params(l, w_in, sbw, nfeat, **weights):
    p = {k: v[l] for k, v in weights.items()}
    w_in_l = w_in[l]
    pq = 3 * sbw + nfeat
    p["w_qkvf"] = w_in_l[:, :pq].astype(BF16)
    p["w_gates"] = w_in_l[:, pq:].astype(BF16)
    for name in ("w_branch_a", "w_branch_b", "w_out"):
        p[name] = p[name].astype(BF16)
    w1, b1 = p.pop("exp_w1"), p.pop("exp_b1")
    ne = w1.shape[0]
    p["w1g"] = w1[:, :, 0::2].astype(BF16)
    p["w1l"] = w1[:, :, 1::2].astype(BF16)
    p["b1g"] = b1[:, 0::2].reshape(ne, 1, -1)
    p["b1l"] = b1[:, 1::2].reshape(ne, 1, -1)
    p["w2"] = p.pop("exp_w2").astype(BF16)
    p["b2"] = p.pop("exp_b2").reshape(ne, 1, -1)
    return p


def _pick_tile(n, want):
    t = min(n, want)
    while n % t:
        t //= 2
    return t


def kernel(x_prompt, x_sample, c_prompt, c_sample, cache_k, cache_v, state_wkv, state_shift, page_table,
           w_ada, b_ada, w_in, sb_bias, rw_mu, rw_w0, rw_w2, rw_a0, rw_a2, rw_g2, rw_kk, rw_ka, rw_rk,
           rw_lnx_g, rw_lnx_b, w_branch_a, w_branch_b, w_out, ln1_g, ln1_b, ln2_g, ln2_b,
           router_w, router_b, exp_w1, exp_b1, exp_w2, exp_b2):
    depth = w_ada.shape[0]
    b, t, d = x_prompt.shape
    db, dt, _ = x_sample.shape
    assert dt == 1, "the sample group advances one token per sequence"
    sb_heads = sb_bias.shape[1]
    sbw = w_branch_a.shape[1]
    sb_hd = sbw // sb_heads
    rw_heads, rw_hd = rw_rk.shape[1], rw_rk.shape[2]
    rww = rw_heads * rw_hd
    nfeat = rw_mu.shape[1]
    n_phys, page = cache_k.shape[1], cache_k.shape[2]
    alpha_res = (2.0 * depth) ** 0.25
    q_scale = sb_hd ** -0.5
    assert t % SBA_TILE == 0 and t % RW_CHUNK == 0

    xp = x_prompt.reshape(b * t, d)
    xs = x_sample.reshape(db, d)
    pad = (-(db + b)) % V7X_SUBLANES
    c_all = jnp.concatenate([c_sample, c_prompt, jnp.zeros((pad, d), F32)], axis=0)
    tm_p = _pick_tile(b * t, 512)
    assert t % tm_p == 0

    outs = [[] for _ in range(8)]
    for l in range(depth):
        p = _layer_params(
            l, w_in, sbw, nfeat, rw_mu=rw_mu, rw_w0=rw_w0, rw_w2=rw_w2, rw_a0=rw_a0, rw_a2=rw_a2, rw_g2=rw_g2,
            rw_kk=rw_kk, rw_ka=rw_ka, rw_rk=rw_rk, w_branch_a=w_branch_a, w_branch_b=w_branch_b, w_out=w_out,
            ln1_g=ln1_g, ln1_b=ln1_b, ln2_g=ln2_g, ln2_b=ln2_b, router_w=router_w, router_b=router_b,
            exp_w1=exp_w1, exp_b1=exp_b1, exp_w2=exp_w2, exp_b2=exp_b2)
        mod = _adaln(c_all, w_ada[l], b_ada[l])
        mods_s = [mod[:db, i * d:(i + 1) * d] for i in range(6)]
        mods_p = [mod[db:db + b, i * d:(i + 1) * d].reshape(b, 1, d) for i in range(6)]
        lng, lnb = rw_lnx_g[l], rw_lnx_b[l]

        kw = dict(per_row=False, toks_per_batch=t, tm=tm_p)
        q, k, v, f = _inproj(xp, mods_p[1], mods_p[0], p["w_qkvf"], sbw=sbw, q_scale=q_scale, **kw)
        o_a = _sba_prompt(q.reshape(b, t, sbw), k.reshape(b, t, sbw), v.reshape(b, t, sbw), sb_bias[l],
                          heads=sb_heads, tile=SBA_TILE).reshape(b * t, sbw)
        f3 = f.reshape(b, t, nfeat)
        prev = jnp.concatenate([jnp.zeros((b, 1, nfeat), F32), f3[:, :-1]], axis=1).reshape(b * t, nfeat)
        ops = _rw_prep_chunk(f, prev, p, tm=_pick_tile(t, 256))
        st0 = jnp.zeros((b, rw_heads, rw_hd, rw_hd), F32)
        o_b, st = _rw_chunk(st0, ops, lng, lnb, batch=b)
        o_b = o_b.transpose(1, 0, 2).reshape(b * t, rww)
        x1, h2, comb = _merge(xp, o_a.astype(BF16), o_b, [mods_p[1], mods_p[0], mods_p[2], mods_p[4], mods_p[3]],
                              p, alpha_res=alpha_res, **kw)
        xp = _moe(h2, comb, x1, mods_p[5], p, alpha_res=alpha_res, **kw)
        outs[0].append(k.reshape(b, t, sb_heads, sb_hd))
        outs[1].append(v.reshape(b, t, sb_heads, sb_hd))
        outs[2].append(st)
        outs[3].append(f3[:, -1])

        kw = dict(per_row=True, toks_per_batch=1, tm=db)
        q, k, v, f = _inproj(xs, mods_s[1], mods_s[0], p["w_qkvf"], sbw=sbw, q_scale=q_scale, **kw)
        o_a = _sba_sample(q.astype(F32), cache_k[l].reshape(n_phys, page, sbw), cache_v[l].reshape(n_phys, page, sbw),
                          page_table, sb_bias[l], heads=sb_heads)
        feats = _rw_prep_step(f, state_shift[l], p)
        o_b, s_new = _rw_step(state_wkv[l], feats, lng, lnb)
        x1, h2, comb = _merge(xs, o_a.astype(BF16), o_b.astype(BF16),
                              [mods_s[1], mods_s[0], mods_s[2], mods_s[4], mods_s[3]], p, alpha_res=alpha_res, **kw)
        xs = _moe(h2, comb, x1, mods_s[5], p, alpha_res=alpha_res, **kw)
        outs[4].append(k.reshape(db, 1, sb_heads, sb_hd))
        outs[5].append(v.reshape(db, 1, sb_heads, sb_hd))
        outs[6].append(s_new)
        outs[7].append(f)

    st = [jnp.stack(o) for o in outs]
    return (xp.reshape(b, t, d), xs.reshape(db, 1, d), st[0], st[1], st[2], st[3], st[4], st[5], st[6], st[7])
```

```python
import functools

import jax
import jax.numpy as jnp
from jax import lax
from jax.experimental import pallas as pl
from jax.experimental.pallas import tpu as pltpu

F32 = jnp.float32
BF16 = jnp.bfloat16

TOP_K = 4
SWIGLU_ALPHA = 1.702
SWIGLU_LIMIT = 7.0
LN_EPS = 1e-5
GN_EPS = 64e-5
KK_NORM_FLOOR = 1e-24
LOG_DECAY_OFFSET = 0.5

V7X_LANES = 128
V7X_SUBLANES = 8
V7X_VMEM_LIMIT_BYTES = 56 * 1024 * 1024

RW_CHUNK = 64
SBA_TILE = 256
SBA_HEAD_GROUP = 4
SBA_PAGES_PER_STEP = 8
MOE_TOKEN_TILE = 2048
MOE_ROW_BLOCK = 320


def _cparams(*sem):
    return pltpu.CompilerParams(dimension_semantics=sem, vmem_limit_bytes=V7X_VMEM_LIMIT_BYTES)


def _dot(a, b):
    return jnp.dot(a.astype(BF16), b.astype(BF16), preferred_element_type=F32)


def _dot_nt(a, b):
    return lax.dot_general(a.astype(BF16), b.astype(BF16), (((1,), (1,)), ((), ())),
                           preferred_element_type=F32)


def _dot_tn(a, b):
    return lax.dot_general(a.astype(BF16), b.astype(BF16), (((0,), (0,)), ((), ())),
                           preferred_element_type=F32)


def _dot_f32(a, b):
    return jnp.dot(a, b, preferred_element_type=F32, precision=lax.Precision.HIGHEST)


def _split3(x):
    hi = x.astype(BF16)
    r1 = x - hi.astype(F32)
    mid = r1.astype(BF16)
    lo = (r1 - mid.astype(F32)).astype(BF16)
    return hi, mid, lo


def _dot_exact_lhs(m_bf16, x):
    hi, mid, lo = _split3(x)
    d = functools.partial(jnp.dot, preferred_element_type=F32)
    return d(m_bf16, hi) + d(m_bf16, mid) + d(m_bf16, lo)


def _dot_exact_rhs(x, m_bf16):
    hi, mid, lo = _split3(x)
    d = functools.partial(jnp.dot, preferred_element_type=F32)
    return d(hi, m_bf16) + d(mid, m_bf16) + d(lo, m_bf16)


def _softplus(z):
    return jnp.maximum(z, 0.0) + jnp.log(1.0 + jnp.exp(-jnp.abs(z)))


def _sigmoid(z):
    return 1.0 / (1.0 + jnp.exp(-z))


def _layer_norm(y, g, b):
    mu = jnp.mean(y, axis=-1, keepdims=True)
    yc = y - mu
    var = jnp.mean(yc * yc, axis=-1, keepdims=True)
    return yc * lax.rsqrt(var + LN_EPS) * g + b


def _iota2(shape, dim):
    return lax.broadcasted_iota(jnp.int32, shape, dim)


def _adaln_kernel(c_ref, w_ref, b_ref, o_ref):
    c = c_ref[...]
    s = c * _sigmoid(c)
    o_ref[...] = _dot(s, w_ref[...]) + b_ref[...]


def _adaln(c, w_ada, b_ada):
    rows, d = c.shape
    n = w_ada.shape[1]
    bn = d
    return pl.pallas_call(
        _adaln_kernel,
        grid=(n // bn,),
        in_specs=[pl.BlockSpec((rows, d), lambda j: (0, 0)),
                  pl.BlockSpec((d, bn), lambda j: (0, j)),
                  pl.BlockSpec((1, bn), lambda j: (0, j))],
        out_specs=pl.BlockSpec((rows, bn), lambda j: (0, j)),
        out_shape=jax.ShapeDtypeStruct((rows, n), F32),
        compiler_params=_cparams("parallel"),
        name="adaln",
    )(c, w_ada, b_ada.reshape(1, n))


def _mod_spec(per_row, tm, d, toks_per_batch):
    if per_row:
        return pl.BlockSpec((tm, d), lambda i: (i, 0))
    return pl.BlockSpec((None, 1, d), lambda i: ((i * tm) // toks_per_batch, 0, 0))


def _inproj_kernel(x_ref, sc_ref, sh_ref, w_ref, q_ref, k_ref, v_ref, f_ref, *, sbw, q_scale):
    h = (x_ref[...] * (1.0 + sc_ref[...]) + sh_ref[...]).astype(BF16)
    d = functools.partial(jnp.dot, preferred_element_type=F32)
    q_ref[...] = (d(h, w_ref[:, 0:sbw]) * q_scale).astype(BF16)
    k_ref[...] = d(h, w_ref[:, sbw:2 * sbw])
    v_ref[...] = d(h, w_ref[:, 2 * sbw:3 * sbw])
    f_ref[...] = d(h, w_ref[:, 3 * sbw:])


def _inproj(x, sc, sh, w_qkvf, *, sbw, q_scale, per_row, toks_per_batch, tm):
    n, d = x.shape
    p = w_qkvf.shape[1]
    nf = p - 3 * sbw
    mspec = _mod_spec(per_row, tm, d, toks_per_batch)
    return pl.pallas_call(
        functools.partial(_inproj_kernel, sbw=sbw, q_scale=q_scale),
        grid=(n // tm,),
        in_specs=[pl.BlockSpec((tm, d), lambda i: (i, 0)), mspec, mspec,
                  pl.BlockSpec((d, p), lambda i: (0, 0))],
        out_specs=[pl.BlockSpec((tm, sbw), lambda i: (i, 0)),
                   pl.BlockSpec((tm, sbw), lambda i: (i, 0)),
                   pl.BlockSpec((tm, sbw), lambda i: (i, 0)),
                   pl.BlockSpec((tm, nf), lambda i: (i, 0))],
        out_shape=[jax.ShapeDtypeStruct((n, sbw), BF16),
                   jax.ShapeDtypeStruct((n, sbw), F32),
                   jax.ShapeDtypeStruct((n, sbw), F32),
                   jax.ShapeDtypeStruct((n, nf), F32)],
        compiler_params=_cparams("parallel"),
        name="inproj",
    )(x, sc, sh, w_qkvf)


def _sba_prompt_kernel(bias_ref, qT_ref, k_ref, vT_ref, oT_ref, carry_scr, acc_scr, *, tile, group):
    hg = pl.program_id(1)
    qi = pl.program_id(2)
    hd = qT_ref.shape[1]
    row = _iota2((tile, tile), 0)
    col = _iota2((tile, tile), 1)
    tri = (col >= row).astype(BF16)
    causal = row < col
    zeros = jnp.zeros((hd, tile), BF16)
    q_pad = [jnp.concatenate([qT_ref[g], zeros] if g % 2 == 0 else [zeros, qT_ref[g]], axis=0)
             for g in range(group)]
    biases = [bias_ref[hg * group + g] for g in range(group)]
    carry_scr[...] = jnp.zeros_like(carry_scr)
    acc_scr[...] = jnp.zeros_like(acc_scr)

    def one_tile(kb, masked):
        gs = range(group)
        dd = functools.partial(jnp.dot, preferred_element_type=F32)
        z = [dd(k_ref[g // 2, kb], q_pad[g]) + biases[g] for g in gs]
        u = [_softplus(z[g]) for g in gs]
        if masked:
            u = [jnp.where(causal, u[g], 0.0) for g in gs]
        c = [dd(tri, u[g].astype(BF16)) for g in gs]
        a = [jnp.exp(z[g] - c[g]) for g in gs]
        if masked:
            a = [jnp.where(causal, a[g], 0.0) for g in gs]
        p = [dd(vT_ref[g, kb], a[g].astype(BF16)) for g in gs]
        for g in gs:
            carry = carry_scr[g]
            acc_scr[g] += p[g] * jnp.exp(-carry)
            carry_scr[g] = carry + c[g][0:1, :]

    one_tile(qi, True)

    def body(j, _):
        one_tile(qi - 1 - j, False)
        return 0

    lax.fori_loop(0, qi, body, 0)
    oT_ref[...] = acc_scr[...].astype(oT_ref.dtype)


def _sba_prompt(q, k, v, bias, *, heads, tile, group):
    b, s, w = q.shape
    hd = w // heads
    nt = s // tile
    k5 = k.astype(BF16).reshape(b, nt, tile, heads // 2, 2 * hd).transpose(0, 3, 1, 2, 4)
    q5 = q.reshape(b, nt, tile, heads, hd).transpose(0, 3, 1, 4, 2)
    v5 = v.astype(BF16).reshape(b, nt, tile, heads, hd).transpose(0, 3, 1, 4, 2)
    oT = pl.pallas_call(
        functools.partial(_sba_prompt_kernel, tile=tile, group=group),
        grid_spec=pltpu.PrefetchScalarGridSpec(
            num_scalar_prefetch=0,
            grid=(b, heads // group, nt),
            in_specs=[pl.BlockSpec(memory_space=pltpu.SMEM),
                      pl.BlockSpec((None, group, None, hd, tile), lambda bi, h, i: (bi, h, i, 0, 0)),
                      pl.BlockSpec((None, group // 2, nt, tile, 2 * hd), lambda bi, h, i: (bi, h, 0, 0, 0)),
                      pl.BlockSpec((None, group, nt, hd, tile), lambda bi, h, i: (bi, h, 0, 0, 0))],
            out_specs=pl.BlockSpec((None, group, None, hd, tile), lambda bi, h, i: (bi, h, i, 0, 0)),
            scratch_shapes=[pltpu.VMEM((group, 1, tile), F32), pltpu.VMEM((group, hd, tile), F32)],
        ),
        out_shape=jax.ShapeDtypeStruct((b, heads, nt, hd, tile), BF16),
        compiler_params=_cparams("parallel", "parallel", "arbitrary"),
        name="sba_prompt",
    )(bias, q5, k5, v5)
    return oT.transpose(0, 2, 4, 1, 3).reshape(b, s, w)


def _sba_sample_kernel(pt_ref, q_ref, bias_ref, *refs, heads, pages_per_step):
    k_refs = refs[:pages_per_step]
    v_refs = refs[pages_per_step:2 * pages_per_step]
    o_ref, carry_ref, acc_ref = refs[2 * pages_per_step:]
    j = pl.program_id(1)
    rows, hd = k_refs[0].shape
    lanes = V7X_LANES
    nblk = rows // lanes

    @pl.when(j == 0)
    def _():
        carry_ref[...] = jnp.zeros_like(carry_ref)
        acc_ref[...] = jnp.zeros_like(acc_ref)

    q = q_ref[...].astype(BF16)
    own = (_iota2((heads, rows), 1) % heads) == _iota2((heads, rows), 0)
    tri = (_iota2((lanes, lanes), 0) >= _iota2((lanes, lanes), 1)).astype(BF16)
    ones = jnp.ones((lanes, lanes), BF16)
    bias = bias_ref[...]
    def suffix_sums(loc, tot):
        off = jnp.zeros((heads, lanes), F32)
        cs = [None] * nblk
        for b in reversed(range(nblk)):
            cs[b] = loc[b * heads:(b + 1) * heads] + off
            off = off + tot[b * heads:(b + 1) * heads]
        return jnp.concatenate(cs, axis=1), off

    ps = range(pages_per_step)
    dd = functools.partial(jnp.dot, preferred_element_type=F32)
    z = [_dot_nt(q, k_refs[i][...]) + bias for i in ps]
    u = [jnp.where(own, _softplus(z[i]), 0.0) for i in ps]
    u_st = [jnp.concatenate([u[i][:, b * lanes:(b + 1) * lanes] for b in range(nblk)], axis=0).astype(BF16)
            for i in ps]
    loc = [dd(u_st[i], tri) for i in ps]
    tot = [dd(u_st[i], ones) for i in ps]
    cs = [suffix_sums(loc[i], tot[i]) for i in ps]
    a = [jnp.where(own, jnp.exp(z[i] - cs[i][0]), 0.0) for i in ps]
    p = [_dot(a[i], v_refs[i][...]) for i in ps]
    carry = carry_ref[...]
    acc = acc_ref[...]
    for i in ps:
        acc = acc + p[i] * jnp.exp(-carry[:, 0:hd])
        carry = carry + cs[i][1]
    acc_ref[...] = acc
    carry_ref[...] = carry

    @pl.when(j == pl.num_programs(1) - 1)
    def _():
        o_ref[...] = acc


def _sba_sample(q, cache_k, cache_v, page_table, bias, *, pages_per_step):
    db, heads, hd = q.shape
    n_pages = page_table.shape[1]
    n_phys, page = cache_k.shape[0], cache_k.shape[1]
    rows = page * heads
    assert n_pages % pages_per_step == 0 and rows % V7X_LANES == 0 and V7X_LANES % heads == 0
    cache_k = cache_k.reshape(n_phys, rows, hd)
    cache_v = cache_v.reshape(n_phys, rows, hd)

    def page_spec(i):
        return pl.BlockSpec((None, rows, hd),
                            lambda b, j, pt: (pt[b, n_pages - 1 - (j * pages_per_step + i)], 0, 0))

    specs = [page_spec(i) for i in range(pages_per_step)]
    return pl.pallas_call(
        functools.partial(_sba_sample_kernel, heads=heads, pages_per_step=pages_per_step),
        grid_spec=pltpu.PrefetchScalarGridSpec(
            num_scalar_prefetch=1,
            grid=(db, n_pages // pages_per_step),
            in_specs=[pl.BlockSpec((None, heads, hd), lambda b, j, pt: (b, 0, 0)),
                      pl.BlockSpec((heads, 1), lambda b, j, pt: (0, 0))] + specs + specs,
            out_specs=pl.BlockSpec((None, heads, hd), lambda b, j, pt: (b, 0, 0)),
            scratch_shapes=[pltpu.VMEM((heads, V7X_LANES), F32), pltpu.VMEM((heads, hd), F32)],
        ),
        out_shape=jax.ShapeDtypeStruct((db, heads, hd), F32),
        compiler_params=_cparams("parallel", "arbitrary"),
        name="sba_sample",
    )(page_table, q, bias.reshape(heads, 1), *([cache_k] * pages_per_step), *([cache_v] * pages_per_step))


def _rw_features(f_ref, prev_ref, mu_ref, w0_ref, w2_ref, a0_ref, a2_ref, g2_ref, kkp_ref, ka_ref, rk_ref,
                 *, rww, rank_w, rank_a, hd):
    f = f_ref[...]
    fm = f + mu_ref[...] * (prev_ref[...] - f)
    fr = fm[:, 0:rww]
    fk = fm[:, rww:2 * rww]
    fv = fm[:, 2 * rww:3 * rww]
    o = 3 * rww
    fw = fm[:, o:o + rank_w]
    fa = fm[:, o + rank_w:o + rank_w + rank_a]
    fg = fm[:, o + rank_w + rank_a:]
    w_log = -_softplus(-(w0_ref[...] + _dot(jnp.tanh(fw), w2_ref[...]))) - LOG_DECAY_OFFSET
    lw = -jnp.exp(w_log)
    a = _sigmoid(a0_ref[...] + _dot(fa, a2_ref[...]))
    g = _dot(_sigmoid(fg), g2_ref[...])
    ones_blk = (_iota2((rww, rww), 0) // hd == _iota2((rww, rww), 1) // hd).astype(BF16)
    kk = fk * kkp_ref[...]
    kk = kk * lax.rsqrt(jnp.maximum(_dot_exact_rhs(kk * kk, ones_blk), KK_NORM_FLOOR))
    k = fk * (1.0 + (a - 1.0) * ka_ref[...])
    bonus = _dot_exact_rhs(fr * k * rk_ref[...], ones_blk) * fv
    return fr, lw, k, fv, kk, kk * a, g, bonus


def _rw_prep_step_kernel(f_ref, prev_ref, mu_ref, w0_ref, w2_ref, a0_ref, a2_ref, g2_ref, kkp_ref, ka_ref, rk_ref,
                         r_ref, w_ref, k_ref, v_ref, kk_ref, b_ref, g_ref, bonus_ref, **dims):
    r, lw, k, v, kk, beta, g, bonus = _rw_features(
        f_ref, prev_ref, mu_ref, w0_ref, w2_ref, a0_ref, a2_ref, g2_ref, kkp_ref, ka_ref, rk_ref, **dims)
    r_ref[...] = r
    w_ref[...] = jnp.exp(lw)
    k_ref[...] = k
    v_ref[...] = v
    kk_ref[...] = kk
    b_ref[...] = beta
    g_ref[...] = g
    bonus_ref[...] = bonus


def _dot_3pass(a, b):
    a_hi = a.astype(BF16)
    b_hi = b.astype(BF16)
    a_lo = (a - a_hi.astype(F32)).astype(BF16)
    b_lo = (b - b_hi.astype(F32)).astype(BF16)
    d = functools.partial(jnp.dot, preferred_element_type=F32)
    return d(a_hi, b_hi) + d(a_hi, b_lo) + d(a_lo, b_hi)


def _unit_lower_inverses(ls, n):
    eye = (_iota2((n, n), 0) == _iota2((n, n), 1)).astype(F32)
    ts = [eye - l for l in ls]
    pws = [_dot(l, l) for l in ls]
    span = 2
    while span < n:
        ts = [t + _dot(t, pw) for t, pw in zip(ts, pws)]
        span *= 2
        if span < n:
            pws = [_dot(pw, pw) for pw in pws]
    resids = [eye - t - _dot_3pass(l, t) for l, t in zip(ls, ts)]
    return [t + _dot(t, r) for t, r in zip(ts, resids)]


def _rw_prep_chunk_kernel(f_ref, prev_ref, mu_ref, w0_ref, w2_ref, a0_ref, a2_ref, g2_ref, kkp_ref, ka_ref, rk_ref,
                          u_ref, w_ref, yl_ref, arb_ref, sk_ref, rt_ref, bb_ref, pc_ref, g_ref, bonus_ref, **dims):
    r, lw, k, v, kk, beta, g, bonus = _rw_features(
        f_ref, prev_ref, mu_ref, w0_ref, w2_ref, a0_ref, a2_ref, g2_ref, kkp_ref, ka_ref, rk_ref, **dims)
    n = r.shape[0]
    hd = dims["hd"]
    heads = r.shape[1] // hd
    row = _iota2((n, n), 0)
    col = _iota2((n, n), 1)
    strict = col < row
    incl = col <= row
    lp = _dot_exact_lhs(incl.astype(BF16), lw)
    lpc = lp[n - 1:n, :]
    e_inv = jnp.exp(-lp)
    e_end = jnp.exp(lpc - lp)
    rt_all = r * jnp.exp(lp)
    at_all = kk * jnp.exp(lp - lw)
    kh_all = k * e_inv
    bh_all = beta * e_inv
    kb_all = k * e_end
    bb_all = beta * e_end
    pc_all = jnp.broadcast_to(jnp.exp(lpc), (pc_ref.shape[1], lp.shape[1]))
    hs = range(heads)
    sls = [slice(h * hd, (h + 1) * hd) for h in hs]
    rt, at, kh, bh, kb, bb, vv = ([x[:, sl].astype(BF16) for sl in sls]
                                  for x in (rt_all, at_all, kh_all, bh_all, kb_all, bb_all, v))
    a_ab = [jnp.where(strict, _dot_nt(at[h], bh[h]), 0.0) for h in hs]
    a_ak = [jnp.where(strict, _dot_nt(at[h], kh[h]), 0.0) for h in hs]
    akv = [_dot(a_ak[h], vv[h]) for h in hs]
    ts = _unit_lower_inverses(a_ab, n)
    for h in hs:
        u_ref[h] = _dot(ts[h], akv[h])
        w_ref[h] = _dot(ts[h], at[h]).astype(w_ref.dtype)
    for h in hs:
        yl_ref[h] = _dot(jnp.where(incl, _dot_nt(rt[h], kh[h]), 0.0), vv[h])
        arb_ref[h] = jnp.where(incl, _dot_nt(rt[h], bh[h]), 0.0).astype(arb_ref.dtype)
        sk_ref[h] = _dot_tn(vv[h], kb[h])
        rt_ref[h] = rt[h]
        bb_ref[h] = bb[h]
        pc_ref[h] = pc_all[:, sls[h]]
        g_ref[h] = g[:, sls[h]]
        bonus_ref[h] = bonus[:, sls[h]]


def _rw_param_specs(nfeat, rww, rank_w, rank_a, rank_g):
    c = lambda i: (0, 0)
    return [pl.BlockSpec((1, nfeat), c), pl.BlockSpec((1, rww), c), pl.BlockSpec((rank_w, rww), c),
            pl.BlockSpec((1, rww), c), pl.BlockSpec((rank_a, rww), c), pl.BlockSpec((rank_g, rww), c),
            pl.BlockSpec((1, rww), c), pl.BlockSpec((1, rww), c), pl.BlockSpec((1, rww), c)]


def _rw_param_args(p):
    rww = p["rw_w0"].shape[0]
    return (p["rw_mu"].reshape(1, -1), p["rw_w0"].reshape(1, rww), p["rw_w2"].astype(BF16),
            p["rw_a0"].reshape(1, rww), p["rw_a2"].astype(BF16), p["rw_g2"].astype(BF16),
            p["rw_kk"].reshape(1, rww), p["rw_ka"].reshape(1, rww), p["rw_rk"].reshape(1, rww))


def _rw_dims(p):
    rww = p["rw_w0"].shape[0]
    return dict(rww=rww, rank_w=p["rw_w2"].shape[0], rank_a=p["rw_a2"].shape[0], hd=p["rw_rk"].shape[1])


def _rw_prep_step(f, prev, p):
    n, nfeat = f.shape
    dims = _rw_dims(p)
    rww = dims["rww"]
    row = pl.BlockSpec((n, rww), lambda i: (0, 0))
    return pl.pallas_call(
        functools.partial(_rw_prep_step_kernel, **dims),
        grid=(1,),
        in_specs=[pl.BlockSpec((n, nfeat), lambda i: (0, 0))] * 2
        + _rw_param_specs(nfeat, rww, dims["rank_w"], dims["rank_a"], p["rw_g2"].shape[0]),
        out_specs=[row] * 8,
        out_shape=[jax.ShapeDtypeStruct((n, rww), F32)] * 8,
        compiler_params=_cparams("arbitrary"),
        name="rw_prep_step",
    )(f, prev, *_rw_param_args(p))


def _rw_prep_chunk(f, prev, p):
    n, nfeat = f.shape
    dims = _rw_dims(p)
    rww, hd = dims["rww"], dims["hd"]
    heads = rww // hd
    tm = RW_CHUNK
    hm = pl.BlockSpec((heads, tm, hd), lambda i: (0, i, 0))
    pcs = pl.BlockSpec((heads, V7X_SUBLANES, hd), lambda i: (0, i, 0))
    dts = [F32, BF16, F32, BF16, F32, BF16, BF16]
    return pl.pallas_call(
        functools.partial(_rw_prep_chunk_kernel, **dims),
        grid=(n // tm,),
        in_specs=[pl.BlockSpec((tm, nfeat), lambda i: (i, 0))] * 2
        + _rw_param_specs(nfeat, rww, dims["rank_w"], dims["rank_a"], p["rw_g2"].shape[0]),
        out_specs=[hm] * 7 + [pcs, hm, hm],
        out_shape=[jax.ShapeDtypeStruct((heads, n, hd), dt) for dt in dts]
        + [jax.ShapeDtypeStruct((heads, n // tm * V7X_SUBLANES, hd), F32)]
        + [jax.ShapeDtypeStruct((heads, n, hd), F32)] * 2,
        compiler_params=_cparams("parallel"),
        name="rw_prep_chunk",
    )(f, prev, *_rw_param_args(p))


def _rw_chunk_kernel(st0_ref, u_ref, w_ref, yl_ref, arb_ref, sk_ref, rt_ref, bb_ref, pc_ref, g_ref, bonus_ref,
                     lng_ref, lnb_ref, o_ref, st_ref, st_scr, *, heads):
    c = pl.program_id(1)

    @pl.when(c == 0)
    def _():
        st_scr[...] = st0_ref[...]

    hs = range(heads)
    s = [st_scr[h] for h in hs]
    x = [-(u_ref[h] + _dot_nt(w_ref[h], s[h])) for h in hs]
    for h in hs:
        st_scr[h] = s[h] * pc_ref[h][0:1, :] + sk_ref[h] + _dot_tn(x[h], bb_ref[h])
    y = [_dot_nt(rt_ref[h], s[h]) + yl_ref[h] + _dot(arb_ref[h], x[h]) for h in hs]
    mu = [jnp.mean(y[h], axis=-1, keepdims=True) for h in hs]
    yc = [y[h] - mu[h] for h in hs]
    var = [jnp.mean(yc[h] * yc[h], axis=-1, keepdims=True) for h in hs]
    for h in hs:
        yn = yc[h] * lax.rsqrt(var[h] + GN_EPS) * lng_ref[h] + lnb_ref[h]
        o_ref[h] = ((yn + bonus_ref[h]) * g_ref[h]).astype(o_ref.dtype)

    @pl.when(c == pl.num_programs(1) - 1)
    def _():
        st_ref[...] = st_scr[...]


def _rw_chunk(st0, ops, lng, lnb, *, batch):
    heads, n, hd = ops[0].shape
    t = n // batch
    nc = t // RW_CHUNK
    blk = pl.BlockSpec((heads, RW_CHUNK, hd), lambda b, c: (0, b * nc + c, 0))
    pcs = pl.BlockSpec((heads, V7X_SUBLANES, hd), lambda b, c: (0, b * nc + c, 0))
    stspec = pl.BlockSpec((None, heads, hd, hd), lambda b, c: (b, 0, 0, 0))
    pspec = pl.BlockSpec((heads, 1, hd), lambda b, c: (0, 0, 0))
    return pl.pallas_call(
        functools.partial(_rw_chunk_kernel, heads=heads),
        grid=(batch, nc),
        in_specs=[stspec] + [blk] * 7 + [pcs, blk, blk] + [pspec, pspec],
        out_specs=[blk, stspec],
        out_shape=[jax.ShapeDtypeStruct((heads, n, hd), BF16),
                   jax.ShapeDtypeStruct((batch, heads, hd, hd), F32)],
        scratch_shapes=[pltpu.VMEM((heads, hd, hd), F32)],
        compiler_params=_cparams("parallel", "arbitrary"),
        name="rw_chunk",
    )(st0, *ops, lng.reshape(heads, 1, hd), lnb.reshape(heads, 1, hd))


def _rw_step_kernel(s_ref, r_ref, w_ref, k_ref, v_ref, kk_ref, b_ref, g_ref, bonus_ref, lng_ref, lnb_ref,
                    o_ref, so_ref, *, heads):
    hd = s_ref.shape[-1]
    eye = _iota2((hd, hd), 0) == _iota2((hd, hd), 1)

    def col_of(rowvec):
        return jnp.sum(jnp.where(eye, jnp.broadcast_to(rowvec, (hd, hd)), 0.0), axis=1, keepdims=True)

    def row_of(colvec):
        return jnp.sum(jnp.where(eye, jnp.broadcast_to(colvec, (hd, hd)), 0.0), axis=0, keepdims=True)

    for h in range(heads):
        sl = slice(h * hd, (h + 1) * hd)
        r, w, k, v, kk, beta = (x[:, sl] for x in (r_ref[...], w_ref[...], k_ref[...], v_ref[...],
                                                     kk_ref[...], b_ref[...]))
        s = s_ref[h]
        s_kk = jnp.sum(s * kk, axis=1, keepdims=True)
        s_new = s * w - s_kk * beta + col_of(v) * k
        so_ref[h] = s_new
        y = row_of(jnp.sum(s_new * r, axis=1, keepdims=True))
        mu = jnp.mean(y, axis=-1, keepdims=True)
        yc = y - mu
        var = jnp.mean(yc * yc, axis=-1, keepdims=True)
        yn = yc * lax.rsqrt(var + GN_EPS) * lng_ref[:, sl] + lnb_ref[:, sl]
        o_ref[:, sl] = (yn + bonus_ref[...][:, sl]) * g_ref[...][:, sl]


def _rw_step(state, feats, lng, lnb):
    db, heads, hd, _ = state.shape
    w = heads * hd
    rowspec = pl.BlockSpec((None, 1, w), lambda b: (b, 0, 0))
    sspec = pl.BlockSpec((None, heads, hd, hd), lambda b: (b, 0, 0, 0))
    pspec = pl.BlockSpec((1, w), lambda b: (0, 0))
    o, s_new = pl.pallas_call(
        functools.partial(_rw_step_kernel, heads=heads),
        grid=(db,),
        in_specs=[sspec] + [rowspec] * 8 + [pspec, pspec],
        out_specs=[rowspec, sspec],
        out_shape=[jax.ShapeDtypeStruct((db, 1, w), F32), jax.ShapeDtypeStruct(state.shape, F32)],
        compiler_params=_cparams("parallel"),
        name="rw_step",
    )(state, *[x.reshape(db, 1, w) for x in feats], lng.reshape(1, w), lnb.reshape(1, w))
    return o.reshape(db, w), s_new


def _merge_kernel(x_ref, oa_ref, ob_ref, scm_ref, shm_ref, gtm_ref, scf_ref, shf_ref,
                  wg_ref, wa_ref, wb_ref, wo_ref, l1g_ref, l1b_ref, rw_ref, rb_ref,
                  x1_ref, h2_ref, comb_ref, *, alpha_res):
    x = x_ref[...]
    d = x.shape[1]
    h = (x * (1.0 + scm_ref[...]) + shm_ref[...]).astype(BF16)
    dd = functools.partial(jnp.dot, preferred_element_type=F32)
    g_a = _sigmoid(dd(h, wg_ref[:, 0:d]))
    g_b = _sigmoid(dd(h, wg_ref[:, d:]))
    merged = g_a * _dot(oa_ref[...], wa_ref[...]) + g_b * _dot(ob_ref[...], wb_ref[...])
    mix = _dot(merged, wo_ref[...])
    x1 = _layer_norm(alpha_res * x + gtm_ref[...] * mix, l1g_ref[...], l1b_ref[...])
    x1_ref[...] = x1
    h2 = x1 * (1.0 + scf_ref[...]) + shf_ref[...]
    h2_ref[...] = h2.astype(BF16)
    logits = _dot_f32(h2, rw_ref[...]) + rb_ref[...]
    ne = logits.shape[1]
    lane = _iota2(logits.shape, 1)
    rest = logits
    picks = []
    for _ in range(TOP_K):
        m = jnp.max(rest, axis=1, keepdims=True)
        first = jnp.min(jnp.where(rest == m, lane, ne), axis=1, keepdims=True)
        sel = lane == first
        picks.append((m, sel))
        rest = jnp.where(sel, -jnp.inf, rest)
    top = picks[0][0]
    es = [jnp.exp(m - top) for m, _ in picks]
    denom = es[0]
    for e in es[1:]:
        denom = denom + e
    comb = jnp.zeros_like(logits)
    for e, (_, sel) in zip(es, picks):
        comb = comb + jnp.where(sel, e / denom, 0.0)
    comb_ref[...] = comb


def _merge(x, o_a, o_b, mods, p, *, alpha_res, per_row, toks_per_batch, tm):
    n, d = x.shape
    sbw = o_a.shape[1]
    rww = o_b.shape[1]
    ne = p["router_w"].shape[1]
    mspec = _mod_spec(per_row, tm, d, toks_per_batch)
    c = lambda i: (0, 0)
    tok = lambda width: pl.BlockSpec((tm, width), lambda i: (i, 0))
    return pl.pallas_call(
        functools.partial(_merge_kernel, alpha_res=alpha_res),
        grid=(n // tm,),
        in_specs=[tok(d), tok(sbw), tok(rww)] + [mspec] * 5
        + [pl.BlockSpec((d, 2 * d), c), pl.BlockSpec((sbw, d), c), pl.BlockSpec((rww, d), c),
           pl.BlockSpec((d, d), c), pl.BlockSpec((1, d), c), pl.BlockSpec((1, d), c),
           pl.BlockSpec((d, ne), c), pl.BlockSpec((1, ne), c)],
        out_specs=[tok(d), tok(d), tok(ne)],
        out_shape=[jax.ShapeDtypeStruct((n, d), F32), jax.ShapeDtypeStruct((n, d), BF16),
                   jax.ShapeDtypeStruct((n, ne), F32)],
        compiler_params=_cparams("parallel"),
        name="merge",
    )(x, o_a, o_b, *mods, p["w_gates"], p["w_branch_a"], p["w_branch_b"], p["w_out"],
      p["ln1_g"].reshape(1, d), p["ln1_b"].reshape(1, d), p["router_w"], p["router_b"].reshape(1, ne))


def _moe_kernel(h_ref, comb_ref, w1_ref, b1_ref, w2_ref, b2_ref, o_ref,
                rank_ref, rank_t_ref, sel_t_ref, w1b_ref, w2x_ref, *, rows, rank_blk):
    e = pl.program_id(1)
    tm, d = h_ref.shape
    ne = comb_ref.shape[1]
    ff2 = w1_ref.shape[1]

    @pl.when(e == 0)
    def _():
        o_ref[...] = jnp.zeros_like(o_ref)
        r_i = _iota2((rank_blk, rank_blk), 0)
        c_i = _iota2((rank_blk, rank_blk), 1)
        before = (c_i < r_i).astype(BF16)
        after = (r_i < c_i).astype(BF16)
        eye = (r_i == c_i).astype(BF16)
        run = jnp.zeros((1, ne), F32)
        run_t = jnp.zeros((ne, 1), F32)
        for b in range(tm // rank_blk):
            sl = slice(b * rank_blk, (b + 1) * rank_blk)
            sel = jnp.where(comb_ref[sl, :] > 0.0, 1.0, 0.0)
            sel_b = sel.astype(BF16)
            rank_ref[sl, :] = jnp.dot(before, sel_b, preferred_element_type=F32) + run
            sel_t = _dot_tn(sel_b, eye)
            sel_t_ref[:, sl] = sel_t
            rank_t_ref[:, sl] = _dot_tn(sel_b, after) + run_t
            run = run + jnp.sum(sel, axis=0, keepdims=True)
            run_t = run_t + jnp.sum(sel_t, axis=1, keepdims=True)

    w1b_ref[...] = w1_ref[...].astype(BF16)
    w2_bits = pltpu.bitcast(w2_ref[...].astype(BF16).astype(F32), jnp.uint32)
    w2x_ref[...] = pltpu.bitcast(lax.shift_right_logical(w2_bits, jnp.uint32(16)), BF16)

    comb = comb_ref[...]
    is_e = _iota2(comb.shape, 1) == e
    ce_col = jnp.sum(jnp.where(is_e, comb, 0.0), axis=1, keepdims=True)
    rank_col = jnp.sum(jnp.where(is_e, rank_ref[...], 0.0), axis=1, keepdims=True)
    rank_col = jnp.where(ce_col > 0.0, rank_col, -1.0)
    sel_row = sel_t_ref[pl.ds(e, 1), :]
    rank_row = jnp.where(sel_row > 0.0, rank_t_ref[pl.ds(e, 1), :], -1.0)
    count = jnp.sum(sel_row).astype(jnp.int32)
    even_lane = (_iota2((rows, ff2), 1) % 2) == 0
    b1 = b1_ref[...]
    b2 = b2_ref[...]

    def block(blk, _):
        base = (blk * rows).astype(F32)
        slot_r = _iota2((rows, tm), 0).astype(F32) + base
        gather = jnp.where(rank_row == slot_r, 1.0, 0.0).astype(BF16)
        xg = jnp.dot(gather, h_ref[...], preferred_element_type=F32).astype(BF16)
        hid = jnp.dot(xg, w1b_ref[...], preferred_element_type=F32) + b1
        nxt = pltpu.roll(hid, ff2 - 1, 1)
        x_glu = jnp.minimum(hid, SWIGLU_LIMIT)
        x_lin = jnp.clip(nxt, -SWIGLU_LIMIT, SWIGLU_LIMIT)
        act = jnp.where(even_lane, x_glu * _sigmoid(SWIGLU_ALPHA * x_glu) * (x_lin + 1.0), 0.0)
        y = jnp.dot(act.astype(BF16), w2x_ref[...], preferred_element_type=F32) + b2
        slot_c = _iota2((tm, rows), 1).astype(F32) + base
        scatter = jnp.where(rank_col == slot_c, 1.0, 0.0).astype(BF16)
        o_ref[...] += ce_col * jnp.dot(scatter, y.astype(BF16), preferred_element_type=F32)
        return 0

    lax.fori_loop(0, (count + rows - 1) // rows, block, 0)


def _moe(h2, comb, p, *, tm, rows):
    n, d = h2.shape
    ne, _, ff2 = p["exp_w1"].shape
    rank_blk = min(tm, 256)
    once = dict(pipeline_mode=pl.Buffered(1))
    tok = lambda width: pl.BlockSpec((tm, width), lambda i, e: (i, 0), **once)
    ex = lambda a, b: pl.BlockSpec((None, a, b), lambda i, e: (e, 0, 0))
    return pl.pallas_call(
        functools.partial(_moe_kernel, rows=rows, rank_blk=rank_blk),
        grid=(n // tm, ne),
        in_specs=[tok(d), tok(ne), ex(d, ff2), ex(1, ff2), ex(ff2 // 2, d), ex(1, d)],
        out_specs=tok(d),
        out_shape=jax.ShapeDtypeStruct((n, d), F32),
        scratch_shapes=[pltpu.VMEM((tm, ne), F32), pltpu.VMEM((ne, tm), F32), pltpu.VMEM((ne, tm), F32),
                        pltpu.VMEM((d, ff2), BF16), pltpu.VMEM((ff2, d), BF16)],
        compiler_params=_cparams("parallel", "arbitrary"),
        name="moe",
    )(h2, comb, p["exp_w1"], p["exp_b1"].reshape(ne, 1, ff2), p["exp_w2"], p["exp_b2"].reshape(ne, 1, d))


def _ffn_out_kernel(x1_ref, moe_ref, gtf_ref, l2g_ref, l2b_ref, o_ref, *, alpha_res):
    o_ref[...] = _layer_norm(alpha_res * x1_ref[...] + gtf_ref[...] * moe_ref[...], l2g_ref[...], l2b_ref[...])


def _ffn_out(x1, moe, gtf, p, *, alpha_res, per_row, toks_per_batch, tm):
    n, d = x1.shape
    tok = pl.BlockSpec((tm, d), lambda i: (i, 0))
    vec = pl.BlockSpec((1, d), lambda i: (0, 0))
    return pl.pallas_call(
        functools.partial(_ffn_out_kernel, alpha_res=alpha_res),
        grid=(n // tm,),
        in_specs=[tok, tok, _mod_spec(per_row, tm, d, toks_per_batch), vec, vec],
        out_specs=tok,
        out_shape=jax.ShapeDtypeStruct((n, d), F32),
        compiler_params=_cparams("parallel"),
        name="ffn_out",
    )(x1, moe, gtf, p["ln2_g"].reshape(1, d), p["ln2_b"].reshape(1, d))


def _layer_params(l, w_in, sbw, nfeat, **weights):
    p = {k: v[l] for k, v in weights.items()}
    w_in_l = w_in[l]
    pq = 3 * sbw + nfeat
    p["w_qkvf"] = w_in_l[:, :pq].astype(BF16)
    p["w_gates"] = w_in_l[:, pq:].astype(BF16)
    for name in ("w_branch_a", "w_branch_b", "w_out"):
        p[name] = p[name].astype(BF16)
    return p


def _pick_tile(n, want):
    t = min(n, want)
    while n % t:
        t //= 2
    return t


def kernel(x_prompt, x_sample, c_prompt, c_sample, cache_k, cache_v, state_wkv, state_shift, page_table,
           w_ada, b_ada, w_in, sb_bias, rw_mu, rw_w0, rw_w2, rw_a0, rw_a2, rw_g2, rw_kk, rw_ka, rw_rk,
           rw_lnx_g, rw_lnx_b, w_branch_a, w_branch_b, w_out, ln1_g, ln1_b, ln2_g, ln2_b,
           router_w, router_b, exp_w1, exp_b1, exp_w2, exp_b2):
    depth = w_ada.shape[0]
    b, t, d = x_prompt.shape
    db, dt, _ = x_sample.shape
    assert dt == 1, "the sample group advances one token per sequence"
    sb_heads = sb_bias.shape[1]
    sbw = w_branch_a.shape[1]
    sb_hd = sbw // sb_heads
    rw_heads, rw_hd = rw_rk.shape[1], rw_rk.shape[2]
    rww = rw_heads * rw_hd
    nfeat = rw_mu.shape[1]
    n_phys, page = cache_k.shape[1], cache_k.shape[2]
    alpha_res = (2.0 * depth) ** 0.25
    q_scale = sb_hd ** -0.5
    assert t % SBA_TILE == 0 and t % RW_CHUNK == 0

    xp = x_prompt.reshape(b * t, d)
    xs = x_sample.reshape(db, d)
    pad = (-(db + b)) % V7X_SUBLANES
    c_all = jnp.concatenate([c_sample, c_prompt, jnp.zeros((pad, d), F32)], axis=0)
    tm_p = _pick_tile(b * t, 512)
    assert t % tm_p == 0

    outs = [[] for _ in range(8)]
    for l in range(depth):
        p = _layer_params(
            l, w_in, sbw, nfeat, rw_mu=rw_mu, rw_w0=rw_w0, rw_w2=rw_w2, rw_a0=rw_a0, rw_a2=rw_a2, rw_g2=rw_g2,
            rw_kk=rw_kk, rw_ka=rw_ka, rw_rk=rw_rk, w_branch_a=w_branch_a, w_branch_b=w_branch_b, w_out=w_out,
            ln1_g=ln1_g, ln1_b=ln1_b, ln2_g=ln2_g, ln2_b=ln2_b, router_w=router_w, router_b=router_b,
            exp_w1=exp_w1, exp_b1=exp_b1, exp_w2=exp_w2, exp_b2=exp_b2)
        mod = _adaln(c_all, w_ada[l], b_ada[l])
        mods_s = [mod[:db, i * d:(i + 1) * d] for i in range(6)]
        mods_p = [mod[db:db + b, i * d:(i + 1) * d].reshape(b, 1, d) for i in range(6)]
        lng, lnb = rw_lnx_g[l], rw_lnx_b[l]

        kw = dict(per_row=False, toks_per_batch=t, tm=tm_p)
        q, k, v, f = _inproj(xp, mods_p[1], mods_p[0], p["w_qkvf"], sbw=sbw, q_scale=q_scale, **kw)
        o_a = _sba_prompt(q.reshape(b, t, sbw), k.reshape(b, t, sbw), v.reshape(b, t, sbw), sb_bias[l],
                          heads=sb_heads, tile=SBA_TILE, group=SBA_HEAD_GROUP).reshape(b * t, sbw)
        f3 = f.reshape(b, t, nfeat)
        prev = jnp.concatenate([jnp.zeros((b, 1, nfeat), F32), f3[:, :-1]], axis=1).reshape(b * t, nfeat)
        ops = _rw_prep_chunk(f, prev, p)
        st0 = jnp.zeros((b, rw_heads, rw_hd, rw_hd), F32)
        o_b, st = _rw_chunk(st0, ops, lng, lnb, batch=b)
        o_b = o_b.transpose(1, 0, 2).reshape(b * t, rww)
        x1, h2, comb = _merge(xp, o_a, o_b, [mods_p[1], mods_p[0], mods_p[2], mods_p[4], mods_p[3]],
                              p, alpha_res=alpha_res, **kw)
        tm_moe = _pick_tile(b * t, MOE_TOKEN_TILE)
        moe = _moe(h2, comb, p, tm=tm_moe, rows=min(MOE_ROW_BLOCK, tm_moe))
        xp = _ffn_out(x1, moe, mods_p[5], p, alpha_res=alpha_res, **kw)
        outs[0].append(k.reshape(b, t, sb_heads, sb_hd))
        outs[1].append(v.reshape(b, t, sb_heads, sb_hd))
        outs[2].append(st)
        outs[3].append(f3[:, -1])

        kw = dict(per_row=True, toks_per_batch=1, tm=db)
        q, k, v, f = _inproj(xs, mods_s[1], mods_s[0], p["w_qkvf"], sbw=sbw, q_scale=q_scale, **kw)
        o_a = _sba_sample(q.astype(F32).reshape(db, sb_heads, sb_hd), cache_k[l], cache_v[l], page_table, sb_bias[l],
                          pages_per_step=SBA_PAGES_PER_STEP).reshape(db, sbw)
        feats = _rw_prep_step(f, state_shift[l], p)
        o_b, s_new = _rw_step(state_wkv[l], feats, lng, lnb)
        x1, h2, comb = _merge(xs, o_a.astype(BF16), o_b.astype(BF16),
                              [mods_s[1], mods_s[0], mods_s[2], mods_s[4], mods_s[3]], p, alpha_res=alpha_res, **kw)
        moe = _moe(h2, comb, p, tm=db, rows=min(MOE_ROW_BLOCK, db))
        xs = _ffn_out(x1, moe, mods_s[5], p, alpha_res=alpha_res, **kw)
        outs[4].append(k.reshape(db, 1, sb_heads, sb_hd))
        outs[5].append(v.reshape(db, 1, sb_heads, sb_hd))
        outs[6].append(s_new)
        outs[7].append(f)

    st = [jnp.stack(o) for o in outs]
    return (xp.reshape(b, t, d), xs.reshape(db, 1, d), st[0], st[1], st[2], st[3], st[4], st[5], st[6], st[7])
```
